```python
import functools
import jax, jax.numpy as jnp
from jax import lax
import numpy as np

D_MODEL = 2048
BATCH = 8
SEQ = 2048
DEPTH = 1
DEC_BATCH = 128
DEC_SEQ = 8
PAST_LEN = 16384
PAGE_SIZE = 128

N_HEADS = 16
QK_NOPE_DIM = 128
QK_ROPE_DIM = 64
V_HEAD_DIM = 128
Q_LORA_RANK = 512
KV_LORA_RANK = 512
ROPE_THETA = 10000.0
Q_BLOCK = 128
ATTN_SCALE = (QK_NOPE_DIM + QK_ROPE_DIM) ** -0.5
LRU_WIDTH = D_MODEL
LRU_BLOCKS = 16
LRU_BLOCK_DIM = LRU_WIDTH // LRU_BLOCKS
CONV_WIDTH = 4
LRU_C = 8.0
N_EXPERTS = 32
TOP_K = 4
D_FF = D_MODEL
SWIGLU_LIMIT = 7.0
SWIGLU_ALPHA = 1.702
EXPERT_BLOCK = 128
RMS_EPS = 1e-6
N_BRANCHES = 2
IN_SPLITS = [Q_LORA_RANK, Q_LORA_RANK + KV_LORA_RANK, Q_LORA_RANK + KV_LORA_RANK + QK_ROPE_DIM,
             Q_LORA_RANK + KV_LORA_RANK + QK_ROPE_DIM + LRU_WIDTH,
             Q_LORA_RANK + KV_LORA_RANK + QK_ROPE_DIM + 2 * LRU_WIDTH]
IN_COLS = IN_SPLITS[-1] + N_BRANCHES * D_MODEL

kernel_name = "hybrid_rglru_mla_moe_step"


def rmsnorm(x, g):
    xf = x.astype(jnp.float32)
    y = xf * lax.rsqrt(jnp.mean(xf * xf, axis=-1, keepdims=True) + RMS_EPS)
    return (y * g.astype(jnp.float32)).astype(x.dtype)


def rope(x, pos):
    half = QK_ROPE_DIM // 2
    inv = ROPE_THETA ** (-jnp.arange(half, dtype=jnp.float32) / half)
    ang = pos.astype(jnp.float32)[:, None] * inv[None, :]
    cos = jnp.cos(ang)[None, :, None, :]
    sin = jnp.sin(ang)[None, :, None, :]
    xf = x.astype(jnp.float32)
    x1, x2 = xf[..., :half], xf[..., half:]
    return jnp.concatenate([x1 * cos - x2 * sin, x2 * cos + x1 * sin], axis=-1).astype(x.dtype)


def latent_scores(q_lat, q_rope, k_lat, k_rope):
    s = jnp.einsum('bqhc,bkc->bhqk', q_lat, k_lat, preferred_element_type=jnp.float32)
    s = s + jnp.einsum('bqhr,bkr->bhqk', q_rope, k_rope, preferred_element_type=jnp.float32)
    return s * ATTN_SCALE


def prompt_attention(q_lat, q_rope, c_kv, k_rope):
    B, S, H, C = q_lat.shape
    nq = S // Q_BLOCK
    ql = q_lat.reshape(B, nq, Q_BLOCK, H, C).swapaxes(0, 1)
    qr = q_rope.reshape(B, nq, Q_BLOCK, H, QK_ROPE_DIM).swapaxes(0, 1)
    k_pos = jnp.arange(S)

    def block(args):
        ql_b, qr_b, i = args
        s = latent_scores(ql_b, qr_b, c_kv, k_rope)
        q_pos = i * Q_BLOCK + jnp.arange(Q_BLOCK)
        s = jnp.where(k_pos[None, :] <= q_pos[:, None], s, -jnp.inf)
        p = jax.nn.softmax(s, axis=-1).astype(c_kv.dtype)
        return jnp.einsum('bhqk,bkc->bqhc', p, c_kv)

    o = lax.map(block, (ql, qr, jnp.arange(nq)))
    return o.swapaxes(0, 1).reshape(B, S, H, C)


def sample_attention(q_lat, q_rope, c_kv_new, k_rope_new, cache_kv, cache_k_rope, page_table, layer):
    DB, Q, H, C = q_lat.shape

    def update(carry, s, k_lat):
        m, l, acc = carry
        m_new = jnp.maximum(m, jnp.max(s, axis=-1))
        corr = jnp.exp(m - m_new)
        p = jnp.exp(s - m_new[..., None])
        l_new = l * corr + jnp.sum(p, axis=-1)
        acc_new = acc * corr[..., None] + jnp.einsum('bhqk,bkc->bhqc', p, k_lat.astype(jnp.float32))
        return (m_new, l_new, acc_new)

    def page_step(carry, phys):
        k_lat = cache_kv[layer, phys]
        k_r = cache_k_rope[layer, phys]
        return update(carry, latent_scores(q_lat, q_rope, k_lat, k_r), k_lat), None

    init = (jnp.full((DB, H, Q), -jnp.inf, jnp.float32),
            jnp.zeros((DB, H, Q), jnp.float32),
            jnp.zeros((DB, H, Q, C), jnp.float32))
    carry, _ = lax.scan(page_step, init, page_table.T)
    causal = jnp.tril(jnp.ones((Q, Q), dtype=bool))
    s_new = jnp.where(causal, latent_scores(q_lat, q_rope, c_kv_new, k_rope_new), -jnp.inf)
    _, l, acc = update(carry, s_new, c_kv_new)
    o = acc / l[..., None]
    return o.transpose(0, 2, 1, 3).astype(q_lat.dtype)


def rglru_branch(x_in, gate_in, conv_buf, h0, w_conv, b_conv, w_a, b_a, w_x, b_x, lru_lambda):
    B, S, W = x_in.shape
    xcat = jnp.concatenate([conv_buf.astype(x_in.dtype), x_in], axis=1)
    xc = b_conv
    for j in range(CONV_WIDTH):
        xc = xc + xcat[:, j:j + S] * w_conv[j]
    new_buf = xcat[:, -(CONV_WIDTH - 1):]
    xb = xc.reshape(B, S, LRU_BLOCKS, LRU_BLOCK_DIM)
    r = jax.nn.sigmoid(jnp.einsum('bsnk,nkj->bsnj', xb, w_a) + b_a).reshape(B, S, W)
    i = jax.nn.sigmoid(jnp.einsum('bsnk,nkj->bsnj', xb, w_x) + b_x).reshape(B, S, W)
    log_a = -LRU_C * r.astype(jnp.float32) * jax.nn.softplus(-lru_lambda.astype(jnp.float32))
    a = jnp.exp(log_a)
    b = jnp.sqrt(-jnp.expm1(2.0 * log_a)) * (i * xc).astype(jnp.float32)
    b = b.at[:, 0].add(a[:, 0] * h0.astype(jnp.float32))

    def combine(lhs, rhs):
        return (lhs[0] * rhs[0], rhs[0] * lhs[1] + rhs[1])

    _, h = lax.associative_scan(combine, (a, b), axis=1)
    y = h.astype(x_in.dtype) * jax.nn.gelu(gate_in)
    return y, h[:, -1].astype(x_in.dtype), new_buf


def moe_ffn(x, layer, w_router, b_router, w_gate_up, b_gate_up, w_down, b_down):
    T, D = x.shape
    logits = (x @ w_router[layer] + b_router[layer]).astype(jnp.float32)
    top_val, top_idx = lax.top_k(logits, TOP_K)
    gate = jax.nn.softmax(top_val, axis=-1)
    A = T * TOP_K
    flat_e = top_idx.reshape(A).astype(jnp.int32)
    flat_t = jnp.repeat(jnp.arange(T, dtype=jnp.int32), TOP_K)
    flat_w = gate.reshape(A)
    order = jnp.argsort(flat_e)
    e_sorted = flat_e[order]
    counts = jnp.bincount(flat_e, length=N_EXPERTS)
    starts = jnp.cumsum(counts) - counts
    padded = ((counts + EXPERT_BLOCK - 1) // EXPERT_BLOCK) * EXPERT_BLOCK
    pend = jnp.cumsum(padded)
    pstart = pend - padded
    dest = pstart[e_sorted] + (jnp.arange(A, dtype=jnp.int32) - starts[e_sorted])
    n_blocks = -(-A // EXPERT_BLOCK) + N_EXPERTS
    P = n_blocks * EXPERT_BLOCK
    row_token = jnp.full((P,), T, jnp.int32).at[dest].set(flat_t[order])
    row_w = jnp.zeros((P,), jnp.float32).at[dest].set(flat_w[order])
    block_expert = jnp.minimum(
        jnp.searchsorted(pend, jnp.arange(n_blocks) * EXPERT_BLOCK, side='right'), N_EXPERTS - 1)
    x_pad = jnp.concatenate([x, jnp.zeros((1, D), x.dtype)], axis=0)
    xs = x_pad[row_token].reshape(n_blocks, EXPERT_BLOCK, D)

    def expert_block(args):
        xb, e = args
        gu = xb @ w_gate_up[layer, e] + b_gate_up[layer, e]
        g, u = gu[:, :D_FF], gu[:, D_FF:]
        g = jnp.minimum(g, SWIGLU_LIMIT)
        u = jnp.clip(u, -SWIGLU_LIMIT, SWIGLU_LIMIT)
        act = (u + 1.0) * (g * jax.nn.sigmoid(SWIGLU_ALPHA * g))
        return act @ w_down[layer, e] + b_down[layer, e]

    ys = lax.map(expert_block, (xs, block_expert)).reshape(P, D)
    ys = ys * row_w[:, None].astype(ys.dtype)
    return jax.ops.segment_sum(ys, row_token, num_segments=T + 1)[:T]


def trunk_layer(x, pos, attend, conv_buf, h0, l, g_mix, w_in, g_q_lat, w_q_up, g_kv_lat, w_kv_up,
                w_o_mla, w_conv, b_conv, w_rg_a, b_rg_a, w_rg_x, b_rg_x, lru_lambda, w_o_lru, w_out,
                g_ffn, w_router, b_router, w_gate_up, b_gate_up, w_down, b_down):
    B, S, _ = x.shape
    h = rmsnorm(x, g_mix[l])
    proj = h @ w_in[l]
    c_q, c_kv, k_r, lru_x, lru_gate, gate_logits = jnp.split(proj, IN_SPLITS, axis=-1)
    c_q = rmsnorm(c_q, g_q_lat[l])
    q = (c_q @ w_q_up[l]).reshape(B, S, N_HEADS, QK_NOPE_DIM + QK_ROPE_DIM)
    q_nope, q_rope = q[..., :QK_NOPE_DIM], rope(q[..., QK_NOPE_DIM:], pos)
    c_kv = rmsnorm(c_kv, g_kv_lat[l])
    k_r = rope(k_r[:, :, None, :], pos)[:, :, 0, :]
    w_kv = w_kv_up[l].reshape(KV_LORA_RANK, N_HEADS, QK_NOPE_DIM + V_HEAD_DIM)
    w_uk, w_uv = w_kv[..., :QK_NOPE_DIM], w_kv[..., QK_NOPE_DIM:]
    q_lat = jnp.einsum('bshd,chd->bshc', q_nope, w_uk)
    o_lat = attend(q_lat, q_rope, c_kv, k_r)
    o = jnp.einsum('bshc,chd->bshd', o_lat, w_uv).reshape(B, S, N_HEADS * V_HEAD_DIM)
    y_mla = o @ w_o_mla[l]
    y_rec, h_new, buf_new = rglru_branch(lru_x, lru_gate, conv_buf, h0, w_conv[l], b_conv[l],
                                         w_rg_a[l], b_rg_a[l], w_rg_x[l], b_rg_x[l], lru_lambda[l])
    y_lru = y_rec @ w_o_lru[l]
    g_lru, g_att = jnp.split(jax.nn.sigmoid(gate_logits), N_BRANCHES, axis=-1)
    x = x + (g_lru * y_lru + g_att * y_mla) @ w_out[l]
    hf = rmsnorm(x, g_ffn[l]).reshape(B * S, D_MODEL)
    x = x + moe_ffn(hf, l, w_router, b_router, w_gate_up, b_gate_up, w_down, b_down).reshape(B, S, D_MODEL)
    return x, c_kv, k_r, h_new, buf_new


def setup_inputs(seed: int = 0) -> dict:
    key = jax.random.key(seed)
    ks = iter(jax.random.split(key, 48))

    def nrm(shape, scale):
        return jax.random.normal(next(ks), shape, jnp.float32) * scale

    def gain(shape):
        return 1.0 + 0.05 * jax.random.normal(next(ks), shape, jnp.float32)

    L = DEPTH
    n_pages = PAST_LEN // PAGE_SIZE
    n_used = DEC_BATCH * n_pages
    n_phys = (n_used * 5) // 4
    x_prompt = nrm((BATCH, SEQ, D_MODEL), 1.0)
    x_sample = nrm((DEC_BATCH, DEC_SEQ, D_MODEL), 1.0)
    cache_kv = nrm((L, n_phys, PAGE_SIZE, KV_LORA_RANK), 1.0)
    cache_k_rope = nrm((L, n_phys, PAGE_SIZE, QK_ROPE_DIM), 1.0)
    state_lru_h = nrm((L, DEC_BATCH, LRU_WIDTH), 0.5)
    state_conv = nrm((L, DEC_BATCH, CONV_WIDTH - 1, LRU_WIDTH), 1.0)
    page_table = jax.random.permutation(next(ks), n_phys)[:n_used].reshape(DEC_BATCH, n_pages).astype(jnp.int32)
    u = jax.random.uniform(next(ks), (L, LRU_WIDTH), jnp.float32, 0.9, 0.999)
    lru_lambda = jnp.log(u) - jnp.log1p(-u)
    return {
        "x_prompt": x_prompt,
        "x_sample": x_sample,
        "cache_kv": cache_kv,
        "cache_k_rope": cache_k_rope,
        "state_lru_h": state_lru_h,
        "state_conv": state_conv,
        "page_table": page_table,
        "g_mix": gain((L, D_MODEL)),
        "w_in": nrm((L, D_MODEL, IN_COLS), D_MODEL ** -0.5),
        "g_q_lat": gain((L, Q_LORA_RANK)),
        "w_q_up": nrm((L, Q_LORA_RANK, N_HEADS * (QK_NOPE_DIM + QK_ROPE_DIM)), Q_LORA_RANK ** -0.5),
        "g_kv_lat": gain((L, KV_LORA_RANK)),
        "w_kv_up": nrm((L, KV_LORA_RANK, N_HEADS * (QK_NOPE_DIM + V_HEAD_DIM)), KV_LORA_RANK ** -0.5),
        "w_o_mla": nrm((L, N_HEADS * V_HEAD_DIM, D_MODEL), (N_HEADS * V_HEAD_DIM) ** -0.5),
        "w_conv": nrm((L, CONV_WIDTH, LRU_WIDTH), CONV_WIDTH ** -0.5),
        "b_conv": nrm((L, LRU_WIDTH), 0.01),
        "w_rg_a": nrm((L, LRU_BLOCKS, LRU_BLOCK_DIM, LRU_BLOCK_DIM), LRU_BLOCK_DIM ** -0.5),
        "b_rg_a": nrm((L, LRU_BLOCKS, LRU_BLOCK_DIM), 0.01),
        "w_rg_x": nrm((L, LRU_BLOCKS, LRU_BLOCK_DIM, LRU_BLOCK_DIM), LRU_BLOCK_DIM ** -0.5),
        "b_rg_x": nrm((L, LRU_BLOCKS, LRU_BLOCK_DIM), 0.01),
        "lru_lambda": lru_lambda,
        "w_o_lru": nrm((L, LRU_WIDTH, D_MODEL), LRU_WIDTH ** -0.5),
        "w_out": nrm((L, D_MODEL, D_MODEL), D_MODEL ** -0.5),
        "g_ffn": gain((L, D_MODEL)),
        "w_router": nrm((L, D_MODEL, N_EXPERTS), D_MODEL ** -0.5),
        "b_router": nrm((L, N_EXPERTS), 0.01),
        "w_gate_up": nrm((L, N_EXPERTS, D_MODEL, 2 * D_FF), D_MODEL ** -0.5),
        "b_gate_up": nrm((L, N_EXPERTS, 2 * D_FF), 0.01),
        "w_down": nrm((L, N_EXPERTS, D_FF, D_MODEL), D_FF ** -0.5),
        "b_down": nrm((L, N_EXPERTS, D_MODEL), 0.01),
        "g_final": gain((D_MODEL,)),
    }


def reference(x_prompt, x_sample, cache_kv, cache_k_rope, state_lru_h, state_conv, page_table,
              g_mix, w_in, g_q_lat, w_q_up, g_kv_lat, w_kv_up, w_o_mla, w_conv, b_conv,
              w_rg_a, b_rg_a, w_rg_x, b_rg_x, lru_lambda, w_o_lru, w_out, g_ffn,
              w_router, b_router, w_gate_up, b_gate_up, w_down, b_down, g_final):
    pos_p = jnp.arange(SEQ, dtype=jnp.int32)
    pos_s = PAST_LEN + jnp.arange(DEC_SEQ, dtype=jnp.int32)
    xp, xs = x_prompt, x_sample
    kv_p, kr_p, h_p, cv_p = [], [], [], []
    kv_s, kr_s, h_s, cv_s = [], [], [], []
    for l in range(DEPTH):
        weights = (g_mix, w_in, g_q_lat, w_q_up, g_kv_lat, w_kv_up, w_o_mla, w_conv, b_conv,
                   w_rg_a, b_rg_a, w_rg_x, b_rg_x, lru_lambda, w_o_lru, w_out, g_ffn,
                   w_router, b_router, w_gate_up, b_gate_up, w_down, b_down)
        conv0 = jnp.zeros((BATCH, CONV_WIDTH - 1, LRU_WIDTH), xp.dtype)
        hzero = jnp.zeros((BATCH, LRU_WIDTH), xp.dtype)
        xp, ckv, kr, hn, cvn = trunk_layer(xp, pos_p, prompt_attention, conv0, hzero, l, *weights)
        kv_p.append(ckv); kr_p.append(kr); h_p.append(hn); cv_p.append(cvn)
        attend_s = functools.partial(sample_attention, cache_kv=cache_kv, cache_k_rope=cache_k_rope,
                                     page_table=page_table, layer=l)
        xs, ckv, kr, hn, cvn = trunk_layer(xs, pos_s, attend_s, state_conv[l], state_lru_h[l], l, *weights)
        kv_s.append(ckv); kr_s.append(kr); h_s.append(hn); cv_s.append(cvn)
    y_prompt = rmsnorm(xp, g_final)
    y_sample = rmsnorm(xs, g_final)
    return (y_prompt, y_sample,
            jnp.stack(kv_p), jnp.stack(kr_p), jnp.stack(h_p), jnp.stack(cv_p),
            jnp.stack(kv_s), jnp.stack(kr_s), jnp.stack(h_s), jnp.stack(cv_s))
```

```python
import functools

import jax
import jax.numpy as jnp
from jax import lax
from jax.experimental import pallas as pl
from jax.experimental.pallas import tpu as pltpu

F32 = jnp.float32
BF16 = jnp.bfloat16

D_MODEL = 2048
N_HEADS = 16
QK_NOPE_DIM = 128
QK_ROPE_DIM = 64
V_HEAD_DIM = 128
Q_LORA_RANK = 512
KV_LORA_RANK = 512
ROPE_THETA = 10000.0
ATTN_SCALE = (QK_NOPE_DIM + QK_ROPE_DIM) ** -0.5
PAST_LEN = 16384
PAGE_SIZE = 128
LRU_WIDTH = D_MODEL
LRU_BLOCK_DIM = 128
CONV_WIDTH = 4
LRU_C = 8.0
N_EXPERTS = 32
TOP_K = 4
D_FF = D_MODEL
SWIGLU_LIMIT = 7.0
SWIGLU_ALPHA = 1.702
RMS_EPS = 1e-6

LANES = 128
HEAD_BLOCK = 2 * LANES
VMEM_LIMIT = 48 * 1024 * 1024

ROW_TILE = 512
IN_TN = 512
MOE_TM = 512
MOE_TF = 512
MOE_TN = 1024
ATTN_T = 256
PAGES_PER_STEP = 8


def _cparams(sem):
    return pltpu.CompilerParams(dimension_semantics=sem, vmem_limit_bytes=VMEM_LIMIT)


def _rms(x, g):
    return x * lax.rsqrt(jnp.mean(x * x, axis=-1, keepdims=True) + RMS_EPS) * g


def _dot(a, b):
    return jnp.dot(a, b, preferred_element_type=F32)


def _dot_nt(a, b):
    return lax.dot_general(a, b, (((1,), (1,)), ((), ())), preferred_element_type=F32)


def _inproj_kernel(x_ref, gmix_ref, w_ref, wkr_ref, gq_ref, gkv_ref, tab_ref,
                   cq_ref, kv_ref, kr_ref, lx_ref, lg_ref, gl_ref, h_scr):
    j = pl.program_id(1)

    @pl.when(j == 0)
    def _():
        hb = _rms(x_ref[...], gmix_ref[...]).astype(BF16)
        h_scr[...] = hb
        z = _dot(hb, wkr_ref[...]) * tab_ref[...]
        kr_ref[...] = z + pltpu.roll(z, QK_ROPE_DIM, 1)

    acc = _dot(h_scr[...], w_ref[...])

    @pl.when(j == 0)
    def _():
        cq_ref[...] = _rms(acc, gq_ref[...]).astype(BF16)

    @pl.when(j == 1)
    def _():
        kv_ref[...] = _rms(acc, gkv_ref[...])

    @pl.when((j >= 2) & (j < 6))
    def _():
        lx_ref[...] = acc

    @pl.when((j >= 6) & (j < 10))
    def _():
        lg_ref[...] = acc

    @pl.when(j >= 10)
    def _():
        gl_ref[...] = acc


def _inproj(x, tab, g_mix, w_main, w_kr, g_q, g_kv, *, seq_len):
    T = x.shape[0]
    tm = ROW_TILE
    ni = T // tm
    nj = w_main.shape[1] // IN_TN
    nlb = D_MODEL // IN_TN
    if seq_len is None:
        lru_shape = (T, D_MODEL)

        def lru_map(first):
            return lambda i, j: (i, jnp.clip(j - first, 0, nlb - 1))
    else:
        spb = seq_len // tm
        lru_shape = (seq_len, (T // seq_len) * D_MODEL)

        def lru_map(first):
            return lambda i, j: (i % spb, (i // spb) * nlb + jnp.clip(j - first, 0, nlb - 1))

    row = lambda i, j: (i, 0)
    const = lambda i, j: (0, 0)
    return pl.pallas_call(
        _inproj_kernel,
        grid=(ni, nj),
        in_specs=[
            pl.BlockSpec((tm, D_MODEL), row),
            pl.BlockSpec((1, D_MODEL), const),
            pl.BlockSpec((D_MODEL, IN_TN), lambda i, j: (0, j)),
            pl.BlockSpec((D_MODEL, LANES), const),
            pl.BlockSpec((1, Q_LORA_RANK), const),
            pl.BlockSpec((1, KV_LORA_RANK), const),
            pl.BlockSpec((tm, LANES), row),
        ],
        out_specs=[
            pl.BlockSpec((tm, Q_LORA_RANK), row),
            pl.BlockSpec((tm, KV_LORA_RANK), row),
            pl.BlockSpec((tm, LANES), row),
            pl.BlockSpec((tm, IN_TN), lru_map(2)),
            pl.BlockSpec((tm, IN_TN), lru_map(2 + nlb)),
            pl.BlockSpec((tm, IN_TN), lambda i, j: (i, jnp.clip(j - 2 - 2 * nlb, 0, 2 * nlb - 1))),
        ],
        out_shape=[
            jax.ShapeDtypeStruct((T, Q_LORA_RANK), BF16),
            jax.ShapeDtypeStruct((T, KV_LORA_RANK), F32),
            jax.ShapeDtypeStruct((T, LANES), F32),
            jax.ShapeDtypeStruct(lru_shape, F32),
            jax.ShapeDtypeStruct(lru_shape, F32),
            jax.ShapeDtypeStruct((T, 2 * D_MODEL), F32),
        ],
        scratch_shapes=[pltpu.VMEM((tm, D_MODEL), BF16)],
        compiler_params=_cparams(("parallel", "arbitrary")),
        name="inproj",
    )(x, g_mix, w_main, w_kr, g_q, g_kv, tab)


def _qproj_kernel(cq_ref, w_ref, ta_ref, tb_ref, q_ref):
    cq = cq_ref[...]
    ta = ta_ref[...]
    tb = tb_ref[...]
    for h in range(N_HEADS):
        lo = h * HEAD_BLOCK
        acc = _dot(cq, w_ref[:, lo:lo + HEAD_BLOCK])
        v = acc[:, LANES:]
        rp = v * ta + pltpu.roll(v, QK_ROPE_DIM, 1) * tb
        q_ref[:, lo:lo + LANES] = (acc[:, :LANES] * ATTN_SCALE).astype(BF16)
        q_ref[:, lo + LANES:lo + HEAD_BLOCK] = rp.astype(BF16)


def _qproj(cq, w_q, tab_a, tab_b):
    T = cq.shape[0]
    tm = ROW_TILE
    row = lambda i: (i, 0)
    return pl.pallas_call(
        _qproj_kernel,
        grid=(T // tm,),
        in_specs=[
            pl.BlockSpec((tm, Q_LORA_RANK), row),
            pl.BlockSpec((Q_LORA_RANK, N_HEADS * HEAD_BLOCK), lambda i: (0, 0)),
            pl.BlockSpec((tm, LANES), row),
            pl.BlockSpec((tm, LANES), row),
        ],
        out_specs=pl.BlockSpec((tm, N_HEADS * HEAD_BLOCK), row),
        out_shape=jax.ShapeDtypeStruct((T, N_HEADS * HEAD_BLOCK), BF16),
        compiler_params=_cparams(("parallel",)),
        name="qproj",
    )(cq, w_q, tab_a, tab_b)


def _kvup_kernel(kv_ref, kr_ref, w_ref, k_ref, v_ref):
    c = kv_ref[...].astype(BF16)
    kr = kr_ref[...].astype(BF16)
    for h in range(N_HEADS):
        lo = h * HEAD_BLOCK
        acc = _dot(c, w_ref[:, lo:lo + HEAD_BLOCK])
        k_ref[:, lo:lo + LANES] = acc[:, :LANES].astype(BF16)
        k_ref[:, lo + LANES:lo + HEAD_BLOCK] = kr
        v_ref[:, h * V_HEAD_DIM:(h + 1) * V_HEAD_DIM] = acc[:, LANES:].astype(BF16)


def _kvup(kv, kr, w_kv):
    T = kv.shape[0]
    tm = ROW_TILE
    row = lambda i: (i, 0)
    return pl.pallas_call(
        _kvup_kernel,
        grid=(T // tm,),
        in_specs=[
            pl.BlockSpec((tm, KV_LORA_RANK), row),
            pl.BlockSpec((tm, LANES), row),
            pl.BlockSpec((KV_LORA_RANK, N_HEADS * HEAD_BLOCK), lambda i: (0, 0)),
        ],
        out_specs=[
            pl.BlockSpec((tm, N_HEADS * HEAD_BLOCK), row),
            pl.BlockSpec((tm, N_HEADS * V_HEAD_DIM), row),
        ],
        out_shape=[
            jax.ShapeDtypeStruct((T, N_HEADS * HEAD_BLOCK), BF16),
            jax.ShapeDtypeStruct((T, N_HEADS * V_HEAD_DIM), BF16),
        ],
        compiler_params=_cparams(("parallel",)),
        name="kvup",
    )(kv, kr, w_kv)


def _pattn_kernel(q_ref, k_ref, v_ref, o_ref, *, seq_len):
    t = ATTN_T
    row_id = lax.broadcasted_iota(jnp.int32, (t, t), 0)
    col_id = lax.broadcasted_iota(jnp.int32, (t, t), 1)

    for qi in range(seq_len // t):
        q = q_ref[qi * t:(qi + 1) * t, :]

        def update(carry, start, diagonal):
            m, l, acc = carry
            k = k_ref[pl.ds(start, t), :]
            v = v_ref[pl.ds(start, t), :]
            s = _dot_nt(q, k)
            if diagonal:
                s = jnp.where(col_id <= row_id, s, -jnp.inf)
            m_new = jnp.maximum(m, jnp.max(s, axis=-1, keepdims=True))
            corr = jnp.exp(m - m_new)
            p = jnp.exp(s - m_new)
            l = l * corr + jnp.sum(p, axis=-1, keepdims=True)
            acc = acc * corr + _dot(p.astype(BF16), v)
            return m_new, l, acc

        carry = (jnp.full((t, 1), -jnp.inf, F32), jnp.zeros((t, 1), F32),
                 jnp.zeros((t, V_HEAD_DIM), F32))
        carry = lax.fori_loop(
            0, qi, lambda ki, c: update(c, pl.multiple_of(ki * t, t), False), carry)
        _, l, acc = update(carry, qi * t, True)
        o_ref[qi * t:(qi + 1) * t, :] = (acc / l).astype(BF16)


def _prompt_attention(q, k, v, *, batch, seq_len):
    return pl.pallas_call(
        functools.partial(_pattn_kernel, seq_len=seq_len),
        grid=(batch, N_HEADS),
        in_specs=[
            pl.BlockSpec((seq_len, HEAD_BLOCK), lambda b, h: (b, h)),
            pl.BlockSpec((seq_len, HEAD_BLOCK), lambda b, h: (b, h)),
            pl.BlockSpec((seq_len, V_HEAD_DIM), lambda b, h: (b, h)),
        ],
        out_specs=pl.BlockSpec((seq_len, V_HEAD_DIM), lambda b, h: (b, h)),
        out_shape=jax.ShapeDtypeStruct((batch * seq_len, N_HEADS * V_HEAD_DIM), BF16),
        compiler_params=_cparams(("parallel", "parallel")),
        name="prompt_attention",
    )(q, k, v)


def _sattn_kernel(pt_ref, ql_ref, qr_ref, kvn_ref, krn_ref, *rest, n_new):
    del pt_ref
    pp = PAGES_PER_STEP
    kv_refs = rest[:pp]
    kr_refs = rest[pp:2 * pp]
    o_ref = rest[2 * pp]
    m_scr, l_scr, acc_scr = rest[2 * pp + 1:]
    c = pl.program_id(1)

    @pl.when(c == 0)
    def _():
        m_scr[...] = jnp.full(m_scr.shape, -jnp.inf, F32)
        l_scr[...] = jnp.zeros(l_scr.shape, F32)
        acc_scr[...] = jnp.zeros(acc_scr.shape, F32)

    ql = ql_ref[...]
    qr = qr_ref[...]

    def update(s, values):
        m_old = m_scr[...]
        m_new = jnp.maximum(m_old, jnp.max(s, axis=-1, keepdims=True))
        corr = jnp.exp(m_old - m_new)
        p = jnp.exp(s - m_new)
        l_scr[...] = l_scr[...] * corr + jnp.sum(p, axis=-1, keepdims=True)
        pb = p.astype(BF16)
        pv = None
        for (lo, hi), kb in values:
            d = _dot(pb[:, lo:hi], kb)
            pv = d if pv is None else pv + d
        acc_scr[...] = acc_scr[...] * corr + pv
        m_scr[...] = m_new

    ks = [kv_refs[i][...].astype(BF16) for i in range(pp)]
    ss = [_dot_nt(ql, ks[i]) + _dot_nt(qr, kr_refs[i][...].astype(BF16)) for i in range(pp)]
    update(jnp.concatenate(ss, axis=1),
           [((i * PAGE_SIZE, (i + 1) * PAGE_SIZE), ks[i]) for i in range(pp)])

    @pl.when(c == pl.num_programs(1) - 1)
    def _():
        pad = PAGE_SIZE - n_new
        kn = jnp.concatenate([kvn_ref[...], jnp.zeros((pad, KV_LORA_RANK), F32)], axis=0).astype(BF16)
        krn = jnp.concatenate([krn_ref[...], jnp.zeros((pad, QK_ROPE_DIM), F32)], axis=0).astype(BF16)
        s = _dot_nt(ql, kn) + _dot_nt(qr, krn)
        q_pos = lax.broadcasted_iota(jnp.int32, s.shape, 0) // N_HEADS
        k_pos = lax.broadcasted_iota(jnp.int32, s.shape, 1)
        update(jnp.where(k_pos <= q_pos, s, -jnp.inf), [((0, PAGE_SIZE), kn)])
        o_ref[...] = acc_scr[...] / l_scr[...]


def _sample_attention(page_table, q_lat, q_rope, kv_new, kr_new, cache_kv, cache_kr):
    nb, rows, _ = q_lat.shape
    n_new = kv_new.shape[1]
    n_pages = page_table.shape[1]
    pp = PAGES_PER_STEP
    pt_flat = page_table.reshape(-1)

    def page_map(i):
        return lambda b, c, pt: (0, pt[b * n_pages + c * pp + i], 0, 0)

    seq = lambda b, c, pt: (b, 0, 0)
    grid_spec = pltpu.PrefetchScalarGridSpec(
        num_scalar_prefetch=1,
        grid=(nb, n_pages // pp),
        in_specs=[
            pl.BlockSpec((None, rows, KV_LORA_RANK), seq),
            pl.BlockSpec((None, rows, QK_ROPE_DIM), seq),
            pl.BlockSpec((None, n_new, KV_LORA_RANK), seq),
            pl.BlockSpec((None, n_new, QK_ROPE_DIM), seq),
        ] + [pl.BlockSpec((None, None, PAGE_SIZE, KV_LORA_RANK), page_map(i)) for i in range(pp)]
          + [pl.BlockSpec((None, None, PAGE_SIZE, QK_ROPE_DIM), page_map(i)) for i in range(pp)],
        out_specs=pl.BlockSpec((None, rows, KV_LORA_RANK), seq),
        scratch_shapes=[
            pltpu.VMEM((rows, 1), F32),
            pltpu.VMEM((rows, 1), F32),
            pltpu.VMEM((rows, KV_LORA_RANK), F32),
        ],
    )
    return pl.pallas_call(
        functools.partial(_sattn_kernel, n_new=n_new),
        grid_spec=grid_spec,
        out_shape=jax.ShapeDtypeStruct((nb, rows, KV_LORA_RANK), F32),
        compiler_params=_cparams(("parallel", "arbitrary")),
        name="sample_attention",
    )(pt_flat, q_lat, q_rope, kv_new, kr_new, *([cache_kv] * pp), *([cache_kr] * pp))


def _headmm_kernel(a_ref, w_ref, o_ref):
    o_ref[...] = _dot(a_ref[...].astype(BF16), w_ref[...]).astype(o_ref.dtype)


def _head_matmul(a, w, *, a_block, a_stride, out_dtype, name):
    T = a.shape[0]
    nh, kdim, ndim = w.shape
    assert kdim == a_block
    return pl.pallas_call(
        _headmm_kernel,
        grid=(nh,),
        in_specs=[
            pl.BlockSpec((T, a_block), lambda h: (0, h * a_stride)),
            pl.BlockSpec((None, kdim, ndim), lambda h: (h, 0, 0)),
        ],
        out_specs=pl.BlockSpec((T, ndim), lambda h: (0, h)),
        out_shape=jax.ShapeDtypeStruct((T, nh * ndim), out_dtype),
        compiler_params=_cparams(("parallel",)),
        name=name,
    )(a, w)


def _softplus(z):
    return jnp.maximum(z, 0.0) + jnp.log1p(jnp.exp(-jnp.abs(z)))


def _lru_kernel(x_ref, g_ref, conv0_ref, h0_ref, wc_ref, bc_ref, wa_ref, ba_ref, wx_ref, bx_ref,
                lam_ref, y_ref, hlast_ref, xcat_scr, a_scr, b_scr, h_scr, *, ts, nb, wc):
    s = pl.program_id(1)
    tail = CONV_WIDTH - 1

    @pl.when(s == 0)
    def _():
        xcat_scr[0:tail] = conv0_ref[...]
        h_scr[...] = h0_ref[...]

    @pl.when(s > 0)
    def _():
        xcat_scr[0:tail] = xcat_scr[ts:ts + tail]

    xcat_scr[tail:tail + ts] = x_ref[...]

    xc = bc_ref[...].reshape(1, 1, wc)
    for j in range(CONV_WIDTH):
        xc = xc + xcat_scr[j:j + ts] * wc_ref[j:j + 1, :].reshape(1, 1, wc)
    xc = xc.reshape(ts * nb, wc)

    for n in range(wc // LRU_BLOCK_DIM):
        lo, hi = n * LRU_BLOCK_DIM, (n + 1) * LRU_BLOCK_DIM
        xn = xc[:, lo:hi]
        xb = xn.astype(BF16)
        r = jax.nn.sigmoid(_dot(xb, wa_ref[n]) + ba_ref[:, lo:hi])
        i = jax.nn.sigmoid(_dot(xb, wx_ref[n]) + bx_ref[:, lo:hi])
        log_a = (-LRU_C) * r * _softplus(-lam_ref[:, lo:hi])
        a_scr[:, lo:hi] = jnp.exp(log_a)
        th = jnp.tanh(log_a)
        b_scr[:, lo:hi] = jnp.sqrt(-2.0 * th / (1.0 - th)) * (i * xn)

    def step(t, h):
        rows = pl.ds(pl.multiple_of(t * nb, nb), nb)
        h = a_scr[rows, :] * h + b_scr[rows, :]
        b_scr[rows, :] = h
        return h

    h = lax.fori_loop(0, ts, step, h_scr[...], unroll=min(ts, 8))
    h_scr[...] = h
    y = b_scr[...] * jax.nn.gelu(g_ref[...].reshape(ts * nb, wc))
    y_ref[...] = y.reshape(ts, nb, wc)

    @pl.when(s == pl.num_programs(1) - 1)
    def _():
        hlast_ref[...] = h


def _lru(x, gate, conv0, h0, w_conv, b_conv, w_a, b_a, w_x, b_x, lam):
    S, nb, W = x.shape
    wc = 512
    ts = max(1, 1024 // nb)
    nblk = wc // LRU_BLOCK_DIM
    blk3 = lambda c, s: (s, 0, c)
    chan = lambda c, s: (0, c)
    return pl.pallas_call(
        functools.partial(_lru_kernel, ts=ts, nb=nb, wc=wc),
        grid=(W // wc, S // ts),
        in_specs=[
            pl.BlockSpec((ts, nb, wc), blk3),
            pl.BlockSpec((ts, nb, wc), blk3),
            pl.BlockSpec((CONV_WIDTH - 1, nb, wc), lambda c, s: (0, 0, c)),
            pl.BlockSpec((nb, wc), chan),
            pl.BlockSpec((CONV_WIDTH, wc), chan),
            pl.BlockSpec((1, wc), chan),
            pl.BlockSpec((nblk, LRU_BLOCK_DIM, LRU_BLOCK_DIM), lambda c, s: (c, 0, 0)),
            pl.BlockSpec((1, wc), chan),
            pl.BlockSpec((nblk, LRU_BLOCK_DIM, LRU_BLOCK_DIM), lambda c, s: (c, 0, 0)),
            pl.BlockSpec((1, wc), chan),
            pl.BlockSpec((1, wc), chan),
        ],
        out_specs=[
            pl.BlockSpec((ts, nb, wc), blk3),
            pl.BlockSpec((nb, wc), chan),
        ],
        out_shape=[
            jax.ShapeDtypeStruct((S, nb, W), F32),
            jax.ShapeDtypeStruct((nb, W), F32),
        ],
        scratch_shapes=[
            pltpu.VMEM((ts + CONV_WIDTH - 1, nb, wc), F32),
            pltpu.VMEM((ts * nb, wc), F32),
            pltpu.VMEM((ts * nb, wc), F32),
            pltpu.VMEM((nb, wc), F32),
        ],
        compiler_params=_cparams(("parallel", "arbitrary")),
        name="rglru",
    )(x, gate, conv0, h0, w_conv, b_conv, w_a, b_a, w_x, b_x, lam)


def _merge_kernel(yl_ref, o_ref, wl_ref, wm_ref, gl_ref, ga_ref, z_ref):
    y_lru = _dot(yl_ref[...].astype(BF16), wl_ref[...])
    y_mla = _dot(o_ref[...], wm_ref[...])
    z = jax.nn.sigmoid(gl_ref[...]) * y_lru + jax.nn.sigmoid(ga_ref[...]) * y_mla
    z_ref[...] = z.astype(BF16)


def _merge(y_rec, o, w_o_lru, w_o_mla, gate_logits, *, seq_len):
    T = o.shape[0]
    tm, tn = ROW_TILE, 512
    nnb = D_MODEL // tn
    if seq_len is None:
        yl_map = lambda i, j: (i, 0)
    else:
        spb = seq_len // tm
        yl_map = lambda i, j: (i % spb, i // spb)
    return pl.pallas_call(
        _merge_kernel,
        grid=(T // tm, nnb),
        in_specs=[
            pl.BlockSpec((tm, D_MODEL), yl_map),
            pl.BlockSpec((tm, D_MODEL), lambda i, j: (i, 0)),
            pl.BlockSpec((D_MODEL, tn), lambda i, j: (0, j)),
            pl.BlockSpec((D_MODEL, tn), lambda i, j: (0, j)),
            pl.BlockSpec((tm, tn), lambda i, j: (i, j)),
            pl.BlockSpec((tm, tn), lambda i, j: (i, nnb + j)),
        ],
        out_specs=pl.BlockSpec((tm, tn), lambda i, j: (i, j)),
        out_shape=jax.ShapeDtypeStruct((T, D_MODEL), BF16),
        compiler_params=_cparams(("parallel", "arbitrary")),
        name="merge",
    )(y_rec, o, w_o_lru, w_o_mla, gate_logits, gate_logits)


def _outproj_kernel(x_ref, z_ref, w_ref, gf_ref, wrh_ref, wrl_ref, br_ref, x1_ref, hf_ref, lg_ref):
    x1 = x_ref[...] + _dot(z_ref[...], w_ref[...])
    x1_ref[...] = x1
    hf = _rms(x1, gf_ref[...])
    hi = hf.astype(BF16)
    hf_ref[...] = hi
    lo = (hf - hi.astype(F32)).astype(BF16)
    wrh = wrh_ref[...]
    lg_ref[...] = (_dot(hi, wrh) + _dot(lo, wrh)) + _dot(hi, wrl_ref[...]) + br_ref[...]


def _outproj(x, z, w_out, g_ffn, wr_hi, wr_lo, b_r):
    T = x.shape[0]
    tm = 256
    row = lambda i: (i, 0)
    const = lambda i: (0, 0)
    return pl.pallas_call(
        _outproj_kernel,
        grid=(T // tm,),
        in_specs=[
            pl.BlockSpec((tm, D_MODEL), row),
            pl.BlockSpec((tm, D_MODEL), row),
            pl.BlockSpec((D_MODEL, D_MODEL), const),
            pl.BlockSpec((1, D_MODEL), const),
            pl.BlockSpec((D_MODEL, LANES), const),
            pl.BlockSpec((D_MODEL, LANES), const),
            pl.BlockSpec((1, LANES), const),
        ],
        out_specs=[
            pl.BlockSpec((tm, D_MODEL), row),
            pl.BlockSpec((tm, D_MODEL), row),
            pl.BlockSpec((tm, LANES), row),
        ],
        out_shape=[
            jax.ShapeDtypeStruct((T, D_MODEL), F32),
            jax.ShapeDtypeStruct((T, D_MODEL), BF16),
            jax.ShapeDtypeStruct((T, LANES), F32),
        ],
        compiler_params=_cparams(("parallel",)),
        name="outproj_router",
    )(x, z, w_out, g_ffn, wr_hi, wr_lo, b_r)


def _moe_up_kernel(be_ref, first_ref, nu_ref, xs_ref, wg_ref, wu_ref, bg_ref, bu_ref, act_ref,
                   wg_scr, wu_scr):
    del be_ref
    i = pl.program_id(1)

    @pl.when(i < nu_ref[0])
    def _():
        @pl.when(first_ref[i] == 1)
        def _():
            wg_scr[...] = wg_ref[...].astype(BF16)
            wu_scr[...] = wu_ref[...].astype(BF16)

        x = xs_ref[...]
        g = jnp.minimum(_dot(x, wg_scr[...]) + bg_ref[...], SWIGLU_LIMIT)
        u = jnp.clip(_dot(x, wu_scr[...]) + bu_ref[...], -SWIGLU_LIMIT, SWIGLU_LIMIT)
        act_ref[...] = ((u + 1.0) * (g * jax.nn.sigmoid(SWIGLU_ALPHA * g))).astype(BF16)


def _moe_down_kernel(be_ref, first_ref, nu_ref, act_ref, wd_ref, bd_ref, rw_ref, ys_ref, wd_scr):
    del be_ref
    i = pl.program_id(1)

    @pl.when(i < nu_ref[0])
    def _():
        @pl.when(first_ref[i] == 1)
        def _():
            wd_scr[...] = wd_ref[...].astype(BF16)

        ys_ref[...] = (_dot(act_ref[...], wd_scr[...]) + bd_ref[...]) * rw_ref[...]


def _moe_experts(xs, row_w, block_expert, first, n_used, w_gate_up, b_gate_up, w_down, b_down):
    P = xs.shape[0]
    nblk = P // MOE_TM
    nf = D_FF // MOE_TF

    def blk(i, nu):
        return jnp.minimum(i, nu[0] - 1)

    up_spec = pltpu.PrefetchScalarGridSpec(
        num_scalar_prefetch=3,
        grid=(nf, nblk),
        in_specs=[
            pl.BlockSpec((MOE_TM, D_MODEL), lambda j, i, be, fi, nu: (blk(i, nu), 0)),
            pl.BlockSpec((None, D_MODEL, MOE_TF), lambda j, i, be, fi, nu: (be[blk(i, nu)], 0, j)),
            pl.BlockSpec((None, D_MODEL, MOE_TF), lambda j, i, be, fi, nu: (be[blk(i, nu)], 0, nf + j)),
            pl.BlockSpec((None, 1, MOE_TF), lambda j, i, be, fi, nu: (be[blk(i, nu)], 0, j)),
            pl.BlockSpec((None, 1, MOE_TF), lambda j, i, be, fi, nu: (be[blk(i, nu)], 0, nf + j)),
        ],
        out_specs=pl.BlockSpec((MOE_TM, MOE_TF), lambda j, i, be, fi, nu: (blk(i, nu), j)),
        scratch_shapes=[pltpu.VMEM((D_MODEL, MOE_TF), BF16), pltpu.VMEM((D_MODEL, MOE_TF), BF16)],
    )
    bgu = b_gate_up.reshape(N_EXPERTS, 1, 2 * D_FF)
    act = pl.pallas_call(
        _moe_up_kernel,
        grid_spec=up_spec,
        out_shape=jax.ShapeDtypeStruct((P, D_FF), BF16),
        compiler_params=_cparams(("arbitrary", "arbitrary")),
        name="moe_gate_up",
    )(block_expert, first, n_used, xs, w_gate_up, w_gate_up, bgu, bgu)

    nn = D_MODEL // MOE_TN
    down_spec = pltpu.PrefetchScalarGridSpec(
        num_scalar_prefetch=3,
        grid=(nn, nblk),
        in_specs=[
            pl.BlockSpec((MOE_TM, D_FF), lambda j, i, be, fi, nu: (blk(i, nu), 0)),
            pl.BlockSpec((None, D_FF, MOE_TN), lambda j, i, be, fi, nu: (be[blk(i, nu)], 0, j)),
            pl.BlockSpec((None, 1, MOE_TN), lambda j, i, be, fi, nu: (be[blk(i, nu)], 0, j)),
            pl.BlockSpec((MOE_TM, 1), lambda j, i, be, fi, nu: (blk(i, nu), 0)),
        ],
        out_specs=pl.BlockSpec((MOE_TM, MOE_TN), lambda j, i, be, fi, nu: (blk(i, nu), j)),
        scratch_shapes=[pltpu.VMEM((D_FF, MOE_TN), BF16)],
    )
    return pl.pallas_call(
        _moe_down_kernel,
        grid_spec=down_spec,
        out_shape=jax.ShapeDtypeStruct((P, D_MODEL), F32),
        compiler_params=_cparams(("arbitrary", "arbitrary")),
        name="moe_down",
    )(block_expert, first, n_used, act, w_down, b_down.reshape(N_EXPERTS, 1, D_MODEL),
      row_w.reshape(P, 1))


def _route(logits):
    T = logits.shape[0]
    A = T * TOP_K
    top_val, top_idx = lax.top_k(logits, TOP_K)
    gate = jax.nn.softmax(top_val, axis=-1)
    flat_e = top_idx.reshape(A).astype(jnp.int32)
    onehot = (flat_e[:, None] == jnp.arange(N_EXPERTS, dtype=jnp.int32)[None, :]).astype(jnp.int32)
    csum = jnp.cumsum(onehot, axis=0)
    rank = jnp.sum(csum * onehot, axis=1) - 1
    counts = csum[-1]
    padded = ((counts + MOE_TM - 1) // MOE_TM) * MOE_TM
    pend = jnp.cumsum(padded)
    pstart = pend - padded
    dest = pstart[flat_e] + rank
    nblk = -(-A // MOE_TM) + N_EXPERTS
    P = nblk * MOE_TM
    flat_t = jnp.repeat(jnp.arange(T, dtype=jnp.int32), TOP_K)
    row_token = jnp.full((P,), T, jnp.int32).at[dest].set(flat_t)
    row_w = jnp.zeros((P,), F32).at[dest].set(gate.reshape(A))
    block_expert = jnp.minimum(
        jnp.searchsorted(pend, jnp.arange(nblk, dtype=jnp.int32) * MOE_TM, side="right"),
        N_EXPERTS - 1).astype(jnp.int32)
    first = jnp.concatenate([jnp.ones((1,), jnp.int32),
                             (block_expert[1:] != block_expert[:-1]).astype(jnp.int32)])
    n_used = (pend[-1] // MOE_TM).astype(jnp.int32).reshape(1)
    return dest.reshape(T, TOP_K), row_token, row_w, block_expert, first, n_used


def _final_kernel(x_ref, m_ref, g_ref, y_ref):
    y_ref[...] = _rms(x_ref[...] + m_ref[...], g_ref[...])


def _final(x1, moe, g_final):
    T = x1.shape[0]
    tm = ROW_TILE
    row = lambda i: (i, 0)
    return pl.pallas_call(
        _final_kernel,
        grid=(T // tm,),
        in_specs=[pl.BlockSpec((tm, D_MODEL), row), pl.BlockSpec((tm, D_MODEL), row),
                  pl.BlockSpec((1, D_MODEL), lambda i: (0, 0))],
        out_specs=pl.BlockSpec((tm, D_MODEL), row),
        out_shape=jax.ShapeDtypeStruct((T, D_MODEL), F32),
        compiler_params=_cparams(("parallel",)),
        name="final_norm",
    )(x1, moe, g_final)


def _rope_tables(pos):
    half = QK_ROPE_DIM // 2
    inv = ROPE_THETA ** (-jnp.arange(half, dtype=F32) / half)
    ang = pos.astype(F32)[:, None] * inv[None, :]
    cos, sin = jnp.cos(ang), jnp.sin(ang)
    cos2 = jnp.concatenate([cos, cos], axis=1)
    sin2 = jnp.concatenate([-sin, sin], axis=1)
    zero = jnp.zeros_like(cos2)
    tab_k = jnp.concatenate([cos2, sin2], axis=1)
    tab_a = jnp.concatenate([cos2, zero], axis=1) * ATTN_SCALE
    tab_b = jnp.concatenate([sin2, zero], axis=1) * ATTN_SCALE
    return tab_k, tab_a, tab_b


def kernel(x_prompt, x_sample, cache_kv, cache_k_rope, state_lru_h, state_conv, page_table, g_mix, w_in, g_q_lat, w_q_up, g_kv_lat, w_kv_up, w_o_mla, w_conv, b_conv, w_rg_a, b_rg_a, w_rg_x, b_rg_x, lru_lambda, w_o_lru, w_out, g_ffn, w_router, b_router, w_gate_up, b_gate_up, w_down, b_down, g_final):
    B, S, _ = x_prompt.shape
    DB, Q, _ = x_sample.shape
    Tp, Ts = B * S, DB * Q
    half = QK_ROPE_DIM // 2
    swap = jnp.concatenate([jnp.arange(half, QK_ROPE_DIM), jnp.arange(half)])

    wi = w_in[0]
    c0, c1 = Q_LORA_RANK + KV_LORA_RANK, Q_LORA_RANK + KV_LORA_RANK + QK_ROPE_DIM
    w_main = jnp.concatenate([wi[:, :c0], wi[:, c1:]], axis=1).astype(BF16)
    w_kr = jnp.concatenate([wi[:, c0:c1], wi[:, c0:c1][:, swap]], axis=1).astype(BF16)
    wq = w_q_up[0].reshape(Q_LORA_RANK, N_HEADS, QK_NOPE_DIM + QK_ROPE_DIM)
    wq_r = wq[..., QK_NOPE_DIM:]
    w_q = jnp.concatenate([wq[..., :QK_NOPE_DIM], wq_r, wq_r[..., swap]], axis=-1)
    w_q = w_q.reshape(Q_LORA_RANK, N_HEADS * HEAD_BLOCK).astype(BF16)
    w_kv = w_kv_up[0].astype(BF16)
    wkv3 = w_kv.reshape(KV_LORA_RANK, N_HEADS, QK_NOPE_DIM + V_HEAD_DIM)
    w_uk_t = wkv3[..., :QK_NOPE_DIM].transpose(1, 2, 0)
    w_uv = wkv3[..., QK_NOPE_DIM:].transpose(1, 0, 2)
    w_o_mla_b = w_o_mla[0].astype(BF16)
    w_o_lru_b = w_o_lru[0].astype(BF16)
    w_out_b = w_out[0].astype(BF16)
    w_a = w_rg_a[0].astype(BF16)
    w_x = w_rg_x[0].astype(BF16)
    b_a = b_rg_a[0].reshape(1, LRU_WIDTH)
    b_x = b_rg_x[0].reshape(1, LRU_WIDTH)
    wr = jnp.pad(w_router[0], ((0, 0), (0, LANES - N_EXPERTS)))
    wr_hi = wr.astype(BF16)
    wr_lo = (wr - wr_hi.astype(F32)).astype(BF16)
    b_r = jnp.pad(b_router[0], (0, LANES - N_EXPERTS)).reshape(1, LANES)

    def token_stage(x, tab_k, tab_a, tab_b, seq_len):
        cq, kv, kr, lx, lg, gl = _inproj(x, tab_k, g_mix, w_main, w_kr, g_q_lat, g_kv_lat,
                                         seq_len=seq_len)
        q = _qproj(cq, w_q, tab_a, tab_b)
        return q, kv, kr, lx, lg, gl

    xp = x_prompt.reshape(Tp, D_MODEL)
    tabs_p = _rope_tables(jnp.tile(jnp.arange(S, dtype=jnp.int32), B))
    q_p, kv_p, kr_p, lx_p, lg_p, gl_p = token_stage(xp, *tabs_p, S)
    k_p, v_p = _kvup(kv_p, kr_p, w_kv)
    o_p = _prompt_attention(q_p, k_p, v_p, batch=B, seq_len=S)
    lx_p3 = lx_p.reshape(S, B, LRU_WIDTH)
    y_rec_p, h_p = _lru(lx_p3, lg_p.reshape(S, B, LRU_WIDTH),
                        jnp.zeros((CONV_WIDTH - 1, B, LRU_WIDTH), F32), jnp.zeros((B, LRU_WIDTH), F32),
                        w_conv[0], b_conv, w_a, b_a, w_x, b_x, lru_lambda)
    z_p = _merge(y_rec_p.reshape(S, B * LRU_WIDTH), o_p, w_o_lru_b, w_o_mla_b, gl_p, seq_len=S)
    x1_p, hf_p, lg_r_p = _outproj(xp, z_p, w_out_b, g_ffn, wr_hi, wr_lo, b_r)

    xs = x_sample.transpose(1, 0, 2).reshape(Ts, D_MODEL)
    tabs_s = _rope_tables(PAST_LEN + jnp.repeat(jnp.arange(Q, dtype=jnp.int32), DB))
    q_s, kv_s, kr_s, lx_s, lg_s, gl_s = token_stage(xs, *tabs_s, None)
    kv_s_b = kv_s.reshape(Q, DB, KV_LORA_RANK).transpose(1, 0, 2)
    kr_s_b = kr_s[:, :QK_ROPE_DIM].reshape(Q, DB, QK_ROPE_DIM).transpose(1, 0, 2)
    q_lat = _head_matmul(q_s, w_uk_t, a_block=QK_NOPE_DIM, a_stride=2, out_dtype=BF16, name="q_latent")
    q_lat = q_lat.reshape(Q, DB, N_HEADS, KV_LORA_RANK).transpose(1, 0, 2, 3)
    q_lat = q_lat.reshape(DB, Q * N_HEADS, KV_LORA_RANK)
    q_r = q_s.reshape(Q, DB, N_HEADS, HEAD_BLOCK)[..., QK_NOPE_DIM:QK_NOPE_DIM + QK_ROPE_DIM]
    q_r = q_r.transpose(1, 0, 2, 3).reshape(DB, Q * N_HEADS, QK_ROPE_DIM)
    o_lat = _sample_attention(page_table, q_lat, q_r, kv_s_b, kr_s_b, cache_kv, cache_k_rope)
    o_lat = o_lat.reshape(DB, Q, N_HEADS, KV_LORA_RANK).transpose(1, 0, 2, 3)
    o_lat = o_lat.reshape(Ts, N_HEADS * KV_LORA_RANK)
    o_s = _head_matmul(o_lat, w_uv, a_block=KV_LORA_RANK, a_stride=1, out_dtype=BF16, name="o_value")
    lx_s3 = lx_s.reshape(Q, DB, LRU_WIDTH)
    y_rec_s, h_s = _lru(lx_s3, lg_s.reshape(Q, DB, LRU_WIDTH),
                        state_conv[0].transpose(1, 0, 2), state_lru_h[0],
                        w_conv[0], b_conv, w_a, b_a, w_x, b_x, lru_lambda)
    z_s = _merge(y_rec_s.reshape(Ts, LRU_WIDTH), o_s, w_o_lru_b, w_o_mla_b, gl_s, seq_len=None)
    x1_s, hf_s, lg_r_s = _outproj(xs, z_s, w_out_b, g_ffn, wr_hi, wr_lo, b_r)

    hf = jnp.concatenate([hf_p, hf_s], axis=0)
    logits = jnp.concatenate([lg_r_p, lg_r_s], axis=0)[:, :N_EXPERTS]
    dest, row_token, row_w, block_expert, first, n_used = _route(logits)
    hf_pad = jnp.concatenate([hf, jnp.zeros((1, D_MODEL), BF16)], axis=0)
    ys = _moe_experts(hf_pad[row_token], row_w, block_expert, first, n_used,
                      w_gate_up[0], b_gate_up[0], w_down[0], b_down[0])
    moe = jnp.sum(ys[dest], axis=1)

    y_p = _final(x1_p, moe[:Tp], g_final.reshape(1, D_MODEL))
    y_s = _final(x1_s, moe[Tp:], g_final.reshape(1, D_MODEL))

    tail = CONV_WIDTH - 1
    return (
        y_p.reshape(B, S, D_MODEL),
        y_s.reshape(Q, DB, D_MODEL).transpose(1, 0, 2),
        kv_p.reshape(1, B, S, KV_LORA_RANK),
        kr_p[:, :QK_ROPE_DIM].reshape(1, B, S, QK_ROPE_DIM),
        h_p[None],
        lx_p3[S - tail:].transpose(1, 0, 2)[None],
        kv_s_b[None],
        kr_s_b[None],
        h_s[None],
        lx_s3[Q - tail:].transpose(1, 0, 2)[None],
    )
```
